```python
import math
import jax, jax.numpy as jnp
from jax import lax
import numpy as np

D_MODEL = 2048
BATCH = 4
SEQ = 4096
DEPTH = 1
DEC_BATCH = 128
DEC_SEQ = 1
PAST_LEN = 16384
PAGE_SIZE = 128

NSA_HEADS = 16
NSA_GROUPS = 2
NSA_HPG = NSA_HEADS // NSA_GROUPS
NSA_DK = 96
NSA_DV = 64
CMP_BLOCK = 32
SEL_BLOCK = 64
TOP_N = 16
WINDOW = 512
MLA_HEADS = 16
Q_LORA = 512
KV_LORA = 512
QK_NOPE = 128
QK_ROPE = 64
V_DIM = 128
ROPE_THETA = 10000.0
NUM_BUCKETS = 32
MAX_DISTANCE = 128
D_FF = 5632
PLE_DIM = 256
NORM_EPS = 1e-6
Q_BLOCK = 128
NEG = -1e30
BIG = 1e9

COL_SIZES = (
    NSA_HEADS * NSA_DK,
    NSA_GROUPS * NSA_DK, NSA_GROUPS * NSA_DV,
    NSA_GROUPS * NSA_DK, NSA_GROUPS * NSA_DV,
    NSA_GROUPS * NSA_DK, NSA_GROUPS * NSA_DV,
    NSA_HEADS * 3,
    Q_LORA, KV_LORA, QK_ROPE,
    2 * D_MODEL,
)
COL_SPLITS = tuple(int(v) for v in np.cumsum(COL_SIZES)[:-1])
D_IN = int(sum(COL_SIZES))

kernel_name = 'nsa_mla_parallel_hybrid_step'


def rmsnorm(x, g):
    xf = x.astype(jnp.float32)
    inv = lax.rsqrt(jnp.mean(xf * xf, axis=-1, keepdims=True) + NORM_EPS)
    return (xf * inv * g.astype(jnp.float32)).astype(x.dtype)


def macaron_half(x, g, w_gate, w_up, w_down):
    h = rmsnorm(x, g)
    return x + 0.5 * ((jax.nn.silu(h @ w_gate) * (h @ w_up)) @ w_down)


def ple_add(x, p, g, w_gate, w_proj):
    gate = jax.nn.sigmoid((rmsnorm(x, g) @ w_gate).astype(jnp.float32)).astype(x.dtype)
    return x + gate * (p @ w_proj)


def rope(x, pos):
    half = x.shape[-1] // 2
    inv_freq = ROPE_THETA ** (-jnp.arange(half, dtype=jnp.float32) / half)
    ang = pos.astype(jnp.float32)[:, None] * inv_freq[None, :]
    cos = jnp.cos(ang)[:, None, :]
    sin = jnp.sin(ang)[:, None, :]
    xf = x.astype(jnp.float32)
    x1, x2 = xf[..., :half], xf[..., half:]
    return jnp.concatenate([x1 * cos - x2 * sin, x2 * cos + x1 * sin], axis=-1).astype(x.dtype)


def t5_bucket(rel):
    n = jnp.maximum(rel, 0)
    max_exact = NUM_BUCKETS // 2
    nf = jnp.maximum(n, 1).astype(jnp.float32)
    large = max_exact + (jnp.log(nf / max_exact) / math.log(MAX_DISTANCE / max_exact)
                         * (NUM_BUCKETS - max_exact)).astype(jnp.int32)
    return jnp.where(n < max_exact, n, jnp.minimum(large, NUM_BUCKETS - 1))


def t5_bias(tpos, spos, table):
    b = table[t5_bucket(tpos[:, None] - spos[None, :])]
    T, S = b.shape[:2]
    return jnp.transpose(b, (2, 0, 1)).reshape(NSA_GROUPS, NSA_HPG, T, S).astype(jnp.float32)


def masked_softmax(logits, mask):
    l = jnp.where(mask, logits, NEG)
    m = jnp.max(l, axis=-1, keepdims=True)
    e = jnp.where(mask, jnp.exp(l - m), 0.0)
    return e / jnp.maximum(jnp.sum(e, axis=-1, keepdims=True), 1e-30)


def window_mask(tpos, spos):
    d = tpos[:, None] - spos[None, :]
    return (spos[None, :] >= 0) & (d >= 0) & (d < WINDOW)


def block_mean(x):
    N, L = x.shape[:2]
    xb = x.reshape(N, L // CMP_BLOCK, CMP_BLOCK, *x.shape[2:])
    return jnp.mean(xb.astype(jnp.float32), axis=2).astype(x.dtype)


def gather_rows(x, rows):
    n_ix = jnp.arange(x.shape[0])[:, None, None, None, None]
    g_ix = jnp.arange(NSA_GROUPS)[None, :, None, None, None]
    return x[n_ix, rows, g_ix]


def nsa_dense_attend(q, k, v, tpos, spos, mask, table):
    logits = jnp.einsum('ntghd,nsgd->nghts', q, k).astype(jnp.float32) * NSA_DK ** -0.5
    p = masked_softmax(logits + t5_bias(tpos, spos, table), mask)
    return jnp.einsum('nghts,nsgd->ntghd', p.astype(v.dtype), v), p


def compressed_and_select(q, kc_b, vc_b, tpos, table):
    n_cmp = kc_b.shape[1]
    ends = jnp.arange(n_cmp, dtype=jnp.int32) * CMP_BLOCK + (CMP_BLOCK - 1)
    mask = ends[None, :] <= tpos[:, None]
    o_cmp, p = nsa_dense_attend(q, kc_b, vc_b, tpos, ends, mask, table)
    ratio = SEL_BLOCK // CMP_BLOCK
    n_sel = n_cmp // ratio
    imp = jnp.sum(p, axis=2)
    imp = jnp.sum(imp.reshape(imp.shape[:-1] + (n_sel, ratio)), axis=-1)
    blk = jnp.arange(n_sel, dtype=jnp.int32)[None, :]
    cur = (tpos // SEL_BLOCK)[:, None]
    valid = blk <= cur
    forced = (blk == 0) | (blk == cur) | (blk == cur - 1)
    score = jnp.where(valid, jnp.where(forced, BIG, imp), NEG)
    vals, sel = lax.top_k(score, min(TOP_N, n_sel))
    return o_cmp, sel.astype(jnp.int32), vals > 0.5 * NEG


def selected_attend(q, kg, vg, spos, ok, tpos, table):
    tp = tpos[None, None, :, None, None]
    mask = ok[..., None] & (spos <= tp)
    table_g = table.reshape(NUM_BUCKETS, NSA_GROUPS, NSA_HPG).transpose(1, 0, 2)
    g_ix = jnp.arange(NSA_GROUPS)[None, :, None, None, None]
    bias = table_g[g_ix, t5_bucket(tp - spos)]
    logits = jnp.einsum('ntghd,ngtkpd->nghtkp', q, kg).astype(jnp.float32) * NSA_DK ** -0.5
    logits = logits + jnp.moveaxis(bias, -1, 2).astype(jnp.float32)
    N, G, H_, T, K, P = logits.shape
    p = masked_softmax(logits.reshape(N, G, H_, T, K * P), mask.reshape(N, G, 1, T, K * P))
    return jnp.einsum('nghtj,ngtjd->ntghd', p.astype(vg.dtype), vg.reshape(N, G, T, K * P, vg.shape[-1]))


def combine_branches(o_cmp, o_sel, o_win, gates):
    N, T = gates.shape[:2]
    g = jax.nn.sigmoid(gates.astype(jnp.float32)).reshape(N, T, NSA_GROUPS, NSA_HPG, 3).astype(o_cmp.dtype)
    o = g[..., 0:1] * o_cmp + g[..., 1:2] * o_sel + g[..., 2:3] * o_win
    return o.reshape(N, T, NSA_HEADS * NSA_DV)


def nsa_prompt(nsa_in, table):
    q, kc, vc, ks, vs, kw, vw, gates = nsa_in
    N, T = q.shape[:2]
    tpos = jnp.arange(T, dtype=jnp.int32)
    o_cmp, sel, ok = compressed_and_select(q, block_mean(kc), block_mean(vc), tpos, table)
    nq = T // Q_BLOCK
    k_top = sel.shape[-1]
    q_blk = q.reshape(N, nq, Q_BLOCK, NSA_GROUPS, NSA_HPG, NSA_DK).transpose(1, 0, 2, 3, 4, 5)
    sel_blk = sel.reshape(N, NSA_GROUPS, nq, Q_BLOCK, k_top).transpose(2, 0, 1, 3, 4)
    ok_blk = ok.reshape(N, NSA_GROUPS, nq, Q_BLOCK, k_top).transpose(2, 0, 1, 3, 4)
    t_blk = tpos.reshape(nq, Q_BLOCK)

    def sel_step(args):
        qb, sb, okb, tb = args
        rows = sb[..., None] * SEL_BLOCK + jnp.arange(SEL_BLOCK, dtype=jnp.int32)
        return selected_attend(qb, gather_rows(ks, rows), gather_rows(vs, rows), rows, okb, tb, table)

    o_sel = lax.map(sel_step, (q_blk, sel_blk, ok_blk, t_blk))
    o_sel = o_sel.transpose(1, 0, 2, 3, 4, 5).reshape(N, T, NSA_GROUPS, NSA_HPG, NSA_DV)

    pad = ((0, 0), (WINDOW, 0), (0, 0), (0, 0))
    kw_pad, vw_pad = jnp.pad(kw, pad), jnp.pad(vw, pad)

    def win_step(args):
        qb, i = args
        start = i * Q_BLOCK
        kb = lax.dynamic_slice_in_dim(kw_pad, start, Q_BLOCK + WINDOW, axis=1)
        vb = lax.dynamic_slice_in_dim(vw_pad, start, Q_BLOCK + WINDOW, axis=1)
        tb = start + jnp.arange(Q_BLOCK, dtype=jnp.int32)
        sb = start - WINDOW + jnp.arange(Q_BLOCK + WINDOW, dtype=jnp.int32)
        return nsa_dense_attend(qb, kb, vb, tb, sb, window_mask(tb, sb), table)[0]

    o_win = lax.map(win_step, (q_blk, jnp.arange(nq, dtype=jnp.int32)))
    o_win = o_win.transpose(1, 0, 2, 3, 4, 5).reshape(N, T, NSA_GROUPS, NSA_HPG, NSA_DV)
    wl = min(WINDOW, T)
    return combine_branches(o_cmp, o_sel, o_win, gates), (kw[:, T - wl:], vw[:, T - wl:])


def nsa_sample(nsa_in, table, c_ck, c_cv, c_sk, c_sv, win_k, win_v, page_table, layer):
    q, kc, vc, ks, vs, kw, vw, gates = nsa_in
    N, T = q.shape[:2]
    tpos = PAST_LEN + jnp.arange(T, dtype=jnp.int32)

    def page_means(phys):
        return block_mean(c_ck[layer, phys]), block_mean(c_cv[layer, phys])

    kc_past, vc_past = lax.map(page_means, page_table.T)
    kc_past = kc_past.transpose(1, 0, 2, 3, 4).reshape(N, -1, NSA_GROUPS, NSA_DK)
    vc_past = vc_past.transpose(1, 0, 2, 3, 4).reshape(N, -1, NSA_GROUPS, NSA_DV)
    t_pad = -(-T // SEL_BLOCK) * SEL_BLOCK
    pad = ((0, 0), (0, t_pad - T), (0, 0), (0, 0))
    kc_all = jnp.concatenate([kc_past, block_mean(jnp.pad(kc, pad))], axis=1)
    vc_all = jnp.concatenate([vc_past, block_mean(jnp.pad(vc, pad))], axis=1)
    o_cmp, sel, ok = compressed_and_select(q, kc_all, vc_all, tpos, table)

    rows = sel[..., None] * SEL_BLOCK + jnp.arange(SEL_BLOCK, dtype=jnp.int32)
    is_new = (rows >= PAST_LEN)[..., None]
    past_row = jnp.minimum(rows, PAST_LEN - 1)
    n_ix = jnp.arange(N)[:, None, None, None, None]
    g_ix = jnp.arange(NSA_GROUPS)[None, :, None, None, None]
    phys = page_table[n_ix, past_row // PAGE_SIZE]
    off = past_row % PAGE_SIZE
    new_row = jnp.clip(rows - PAST_LEN, 0, t_pad - 1)
    kg = jnp.where(is_new, gather_rows(jnp.pad(ks, pad), new_row), c_sk[layer, phys, off, g_ix])
    vg = jnp.where(is_new, gather_rows(jnp.pad(vs, pad), new_row), c_sv[layer, phys, off, g_ix])
    o_sel = selected_attend(q, kg, vg, rows, ok, tpos, table)

    wl = win_k.shape[1]
    kw_all = jnp.concatenate([win_k, kw], axis=1)
    vw_all = jnp.concatenate([win_v, vw], axis=1)
    spos = PAST_LEN - wl + jnp.arange(wl + T, dtype=jnp.int32)
    o_win = nsa_dense_attend(q, kw_all, vw_all, tpos, spos, window_mask(tpos, spos), table)[0]
    return combine_branches(o_cmp, o_sel, o_win, gates), (kw_all[:, T:], vw_all[:, T:])


def mla_logits(ql, qr, kc, kr):
    lg = (jnp.einsum('nthc,nsc->nhts', ql, kc).astype(jnp.float32)
          + jnp.einsum('nthr,nsr->nhts', qr, kr).astype(jnp.float32))
    return lg * (QK_NOPE + QK_ROPE) ** -0.5


def mla_prompt(ql, qr, ckv, kr):
    N, T = ql.shape[:2]
    nq = T // Q_BLOCK
    spos = jnp.arange(T, dtype=jnp.int32)
    ql_b = ql.reshape(N, nq, Q_BLOCK, MLA_HEADS, KV_LORA).transpose(1, 0, 2, 3, 4)
    qr_b = qr.reshape(N, nq, Q_BLOCK, MLA_HEADS, QK_ROPE).transpose(1, 0, 2, 3, 4)

    def step(args):
        qlb, qrb, i = args
        tb = i * Q_BLOCK + jnp.arange(Q_BLOCK, dtype=jnp.int32)
        p = masked_softmax(mla_logits(qlb, qrb, ckv, kr), spos[None, :] <= tb[:, None])
        return jnp.einsum('nhts,nsc->nthc', p.astype(ckv.dtype), ckv)

    out = lax.map(step, (ql_b, qr_b, jnp.arange(nq, dtype=jnp.int32)))
    return out.transpose(1, 0, 2, 3, 4).reshape(N, T, MLA_HEADS, KV_LORA)


def mla_sample(ql, qr, ckv, kr, c_ckv, c_kr, page_table, layer):
    T = ql.shape[1]
    causal = jnp.arange(T)[None, :] <= jnp.arange(T)[:, None]
    lg0 = jnp.where(causal, mla_logits(ql, qr, ckv, kr), NEG)
    m0 = jnp.max(lg0, axis=-1, keepdims=True)
    e0 = jnp.where(causal, jnp.exp(lg0 - m0), 0.0)
    carry0 = (m0, jnp.sum(e0, axis=-1, keepdims=True),
              jnp.einsum('nhts,nsc->nhtc', e0, ckv.astype(jnp.float32)))

    def step(carry, phys):
        m, s, acc = carry
        kc = c_ckv[layer, phys]
        lg = mla_logits(ql, qr, kc, c_kr[layer, phys])
        m_new = jnp.maximum(m, jnp.max(lg, axis=-1, keepdims=True))
        corr = jnp.exp(m - m_new)
        e = jnp.exp(lg - m_new)
        return (m_new, s * corr + jnp.sum(e, axis=-1, keepdims=True),
                acc * corr + jnp.einsum('nhts,nsc->nhtc', e, kc.astype(jnp.float32))), None

    (m, s, acc), _ = lax.scan(step, carry0, page_table.T)
    return (acc / s).transpose(0, 2, 1, 3).astype(ckv.dtype)


def mixer_projections(x, pos, g_mix, w_in, g_q, w_qb, g_kv, w_kb):
    N, T, _ = x.shape
    h = rmsnorm(x, g_mix)
    q_n, kc, vc, ks, vs, kw, vw, gn, cq, ckv, kr, gm = jnp.split(h @ w_in, COL_SPLITS, axis=-1)
    kd = (N, T, NSA_GROUPS, NSA_DK)
    vd = (N, T, NSA_GROUPS, NSA_DV)
    nsa_in = (q_n.reshape(N, T, NSA_GROUPS, NSA_HPG, NSA_DK), kc.reshape(kd), vc.reshape(vd),
              ks.reshape(kd), vs.reshape(vd), kw.reshape(kd), vw.reshape(vd),
              gn.reshape(N, T, NSA_HEADS, 3))
    q = (rmsnorm(cq, g_q) @ w_qb).reshape(N, T, MLA_HEADS, QK_NOPE + QK_ROPE)
    q_lat = jnp.einsum('nthd,chd->nthc', q[..., :QK_NOPE], w_kb)
    q_rope = rope(q[..., QK_NOPE:], pos)
    k_rope = rope(kr[:, :, None, :], pos)[:, :, 0, :]
    mla_in = (q_lat, q_rope, rmsnorm(ckv, g_kv), k_rope)
    return nsa_in, mla_in, gm


def merge_out(o_nsa, o_lat, gm, w_vb, w_nsa_out, w_mla_out, w_o):
    N, T = gm.shape[:2]
    v = jnp.einsum('nthc,chd->nthd', o_lat, w_vb).reshape(N, T, MLA_HEADS * V_DIM)
    g = jax.nn.sigmoid(gm.astype(jnp.float32)).astype(gm.dtype)
    return (g[..., :D_MODEL] * (o_nsa @ w_nsa_out) + g[..., D_MODEL:] * (v @ w_mla_out)) @ w_o


def setup_inputs(seed: int = 0) -> dict:
    key = jax.random.key(seed)
    k = jax.random.split(key, 36)
    f32 = jnp.float32
    n_pages = PAST_LEN // PAGE_SIZE
    n_used = DEC_BATCH * n_pages
    n_pool = n_used + max(1, n_used // 4)
    wl = min(WINDOW, PAST_LEN)
    L = DEPTH
    pool = (L, n_pool, PAGE_SIZE)

    def nrm(kk, shape, scale=1.0):
        return scale * jax.random.normal(kk, shape, f32)

    def gain(kk, shape):
        return 1.0 + 0.01 * jax.random.normal(kk, shape, f32)

    return {
        'x_prompt': nrm(k[0], (BATCH, SEQ, D_MODEL)),
        'x_sample': nrm(k[1], (DEC_BATCH, DEC_SEQ, D_MODEL)),
        'cache_mla_ckv': nrm(k[2], pool + (KV_LORA,)),
        'cache_mla_krope': nrm(k[3], pool + (QK_ROPE,)),
        'cache_nsa_cmp_k': nrm(k[4], pool + (NSA_GROUPS, NSA_DK)),
        'cache_nsa_cmp_v': nrm(k[5], pool + (NSA_GROUPS, NSA_DV)),
        'cache_nsa_sel_k': nrm(k[6], pool + (NSA_GROUPS, NSA_DK)),
        'cache_nsa_sel_v': nrm(k[7], pool + (NSA_GROUPS, NSA_DV)),
        'state_win_k': nrm(k[8], (L, DEC_BATCH, wl, NSA_GROUPS, NSA_DK)),
        'state_win_v': nrm(k[9], (L, DEC_BATCH, wl, NSA_GROUPS, NSA_DV)),
        'page_table': jax.random.permutation(k[10], n_pool)[:n_used].reshape(DEC_BATCH, n_pages).astype(jnp.int32),
        'p_prompt': nrm(k[11], (L, BATCH, SEQ, PLE_DIM)),
        'p_sample': nrm(k[12], (L, DEC_BATCH, DEC_SEQ, PLE_DIM)),
        'rel_bias': nrm(k[13], (NUM_BUCKETS, NSA_HEADS), 0.5),
        'g_ffn1': gain(k[14], (L, D_MODEL)),
        'w_ffn1_gate': nrm(k[15], (L, D_MODEL, D_FF), D_MODEL ** -0.5),
        'w_ffn1_up': nrm(k[16], (L, D_MODEL, D_FF), D_MODEL ** -0.5),
        'w_ffn1_down': nrm(k[17], (L, D_FF, D_MODEL), D_FF ** -0.5),
        'g_mix': gain(k[18], (L, D_MODEL)),
        'w_in': nrm(k[19], (L, D_MODEL, D_IN), D_MODEL ** -0.5),
        'g_q': gain(k[20], (L, Q_LORA)),
        'w_qb': nrm(k[21], (L, Q_LORA, MLA_HEADS * (QK_NOPE + QK_ROPE)), Q_LORA ** -0.5),
        'g_kv': gain(k[22], (L, KV_LORA)),
        'w_kb': nrm(k[23], (L, KV_LORA, MLA_HEADS, QK_NOPE), KV_LORA ** -0.5),
        'w_vb': nrm(k[24], (L, KV_LORA, MLA_HEADS, V_DIM), KV_LORA ** -0.5),
        'w_nsa_out': nrm(k[25], (L, NSA_HEADS * NSA_DV, D_MODEL), (NSA_HEADS * NSA_DV) ** -0.5),
        'w_mla_out': nrm(k[26], (L, MLA_HEADS * V_DIM, D_MODEL), (MLA_HEADS * V_DIM) ** -0.5),
        'w_o': nrm(k[27], (L, D_MODEL, D_MODEL), D_MODEL ** -0.5),
        'g_ffn2': gain(k[28], (L, D_MODEL)),
        'w_ffn2_gate': nrm(k[29], (L, D_MODEL, D_FF), D_MODEL ** -0.5),
        'w_ffn2_up': nrm(k[30], (L, D_MODEL, D_FF), D_MODEL ** -0.5),
        'w_ffn2_down': nrm(k[31], (L, D_FF, D_MODEL), D_FF ** -0.5),
        'g_ple': gain(k[32], (L, D_MODEL)),
        'w_ple_gate': nrm(k[33], (L, D_MODEL, D_MODEL), D_MODEL ** -0.5),
        'w_ple': nrm(k[34], (L, PLE_DIM, D_MODEL), PLE_DIM ** -0.5),
        'g_final': gain(k[35], (D_MODEL,)),
    }


def reference(x_prompt, x_sample, cache_mla_ckv, cache_mla_krope, cache_nsa_cmp_k, cache_nsa_cmp_v,
              cache_nsa_sel_k, cache_nsa_sel_v, state_win_k, state_win_v, page_table, p_prompt, p_sample,
              rel_bias, g_ffn1, w_ffn1_gate, w_ffn1_up, w_ffn1_down, g_mix, w_in, g_q, w_qb, g_kv, w_kb, w_vb,
              w_nsa_out, w_mla_out, w_o, g_ffn2, w_ffn2_gate, w_ffn2_up, w_ffn2_down, g_ple, w_ple_gate, w_ple,
              g_final):
    pos_p = jnp.arange(x_prompt.shape[1], dtype=jnp.int32)
    pos_s = PAST_LEN + jnp.arange(x_sample.shape[1], dtype=jnp.int32)
    xp, xs = x_prompt, x_sample
    rows_p, rows_s = [], []
    for i in range(DEPTH):
        ffn1 = (g_ffn1[i], w_ffn1_gate[i], w_ffn1_up[i], w_ffn1_down[i])
        ffn2 = (g_ffn2[i], w_ffn2_gate[i], w_ffn2_up[i], w_ffn2_down[i])
        proj = (g_mix[i], w_in[i], g_q[i], w_qb[i], g_kv[i], w_kb[i])
        outp = (w_vb[i], w_nsa_out[i], w_mla_out[i], w_o[i])
        ple = (g_ple[i], w_ple_gate[i], w_ple[i])

        xp = macaron_half(xp, *ffn1)
        nsa_in, mla_in, gm = mixer_projections(xp, pos_p, *proj)
        o_nsa, win_p = nsa_prompt(nsa_in, rel_bias)
        o_lat = mla_prompt(*mla_in)
        xp = xp + merge_out(o_nsa, o_lat, gm, *outp)
        xp = macaron_half(xp, *ffn2)
        xp = ple_add(xp, p_prompt[i], *ple)
        rows_p.append((mla_in[2], mla_in[3], nsa_in[1], nsa_in[2], nsa_in[3], nsa_in[4], win_p[0], win_p[1]))

        xs = macaron_half(xs, *ffn1)
        nsa_in, mla_in, gm = mixer_projections(xs, pos_s, *proj)
        o_nsa, win_s = nsa_sample(nsa_in, rel_bias, cache_nsa_cmp_k, cache_nsa_cmp_v, cache_nsa_sel_k,
                                  cache_nsa_sel_v, state_win_k[i], state_win_v[i], page_table, i)
        o_lat = mla_sample(*mla_in, cache_mla_ckv, cache_mla_krope, page_table, i)
        xs = xs + merge_out(o_nsa, o_lat, gm, *outp)
        xs = macaron_half(xs, *ffn2)
        xs = ple_add(xs, p_sample[i], *ple)
        rows_s.append((mla_in[2], mla_in[3], nsa_in[1], nsa_in[2], nsa_in[3], nsa_in[4], win_s[0], win_s[1]))

    sp = [jnp.stack(a, axis=0) for a in zip(*rows_p)]
    ss = [jnp.stack(a, axis=0) for a in zip(*rows_s)]
    return (rmsnorm(xp, g_final), rmsnorm(xs, g_final),
            sp[0], sp[1], sp[2], sp[3], sp[4], sp[5], sp[6], sp[7],
            ss[0], ss[1], ss[2], ss[3], ss[4], ss[5], ss[6], ss[7])
```

```python
import functools
import math

import jax
import jax.numpy as jnp
import numpy as np
from jax import lax
from jax.experimental import pallas as pl
from jax.experimental.pallas import tpu as pltpu

D_MODEL = 2048
BATCH = 4
SEQ = 4096
DEPTH = 1
DEC_BATCH = 128
DEC_SEQ = 1
PAST_LEN = 16384
PAGE_SIZE = 128

NSA_HEADS = 16
NSA_GROUPS = 2
NSA_HPG = NSA_HEADS // NSA_GROUPS
NSA_DK = 96
NSA_DV = 64
CMP_BLOCK = 32
SEL_BLOCK = 64
TOP_N = 16
WINDOW = 512
MLA_HEADS = 16
Q_LORA = 512
KV_LORA = 512
QK_NOPE = 128
QK_ROPE = 64
V_DIM = 128
ROPE_THETA = 10000.0
NUM_BUCKETS = 32
MAX_DISTANCE = 128
D_FF = 5632
PLE_DIM = 256
NORM_EPS = 1e-6
Q_BLOCK = 128
NEG = -1e30
BIG = 1e9

LANE = 128
QCAT = KV_LORA + LANE
VMEM_LIMIT = 56 * 2**20

f32 = jnp.float32
bf16 = jnp.bfloat16
i32 = jnp.int32


def _col_sizes():
    return (NSA_HEADS * NSA_DK, NSA_GROUPS * NSA_DK, NSA_GROUPS * NSA_DV, NSA_GROUPS * NSA_DK,
            NSA_GROUPS * NSA_DV, NSA_GROUPS * NSA_DK, NSA_GROUPS * NSA_DV, NSA_HEADS * 3,
            Q_LORA, KV_LORA, QK_ROPE, 2 * D_MODEL)


def _nn(a, b):
    return jnp.dot(a, b, preferred_element_type=f32)


def _nt(a, b):
    return lax.dot_general(a, b, (((1,), (1,)), ((), ())), preferred_element_type=f32)


def _rms(xf, g):
    inv = lax.rsqrt(jnp.mean(xf * xf, axis=-1, keepdims=True) + NORM_EPS)
    return xf * inv * g


def _params(sem):
    return pltpu.CompilerParams(dimension_semantics=sem, vmem_limit_bytes=VMEM_LIMIT)


def _bs(shape, imap):
    return pl.BlockSpec(shape, imap)


def _ffn_body(x_ref, g_ref, wg_ref, wu_ref, wd_ref, gn_ref, o_ref, hn_ref, h_sc, acc_sc):
    f = pl.program_id(1)

    @pl.when(f == 0)
    def _():
        h_sc[...] = _rms(x_ref[...], g_ref[...]).astype(bf16)
        acc_sc[...] = jnp.zeros_like(acc_sc)

    h = h_sc[...]
    gt = _nn(h, wg_ref[...])
    up = _nn(h, wu_ref[...])
    a = (gt * jax.nn.sigmoid(gt) * up).astype(bf16)
    acc_sc[...] += _nn(a, wd_ref[...])

    @pl.when(f == pl.num_programs(1) - 1)
    def _():
        y = x_ref[...] + 0.5 * acc_sc[...]
        o_ref[...] = y
        hn_ref[...] = _rms(y, gn_ref[...]).astype(bf16)


def _ffn(x, g, wg, wu, wd, g_next, tm):
    M, D = x.shape
    F = wg.shape[1]
    tf = min(512, F)
    return pl.pallas_call(
        _ffn_body,
        grid=(M // tm, F // tf),
        in_specs=[_bs((tm, D), lambda i, f: (i, 0)), _bs((1, D), lambda i, f: (0, 0)),
                  _bs((D, tf), lambda i, f: (0, f)), _bs((D, tf), lambda i, f: (0, f)),
                  _bs((tf, D), lambda i, f: (f, 0)), _bs((1, D), lambda i, f: (0, 0))],
        out_specs=[_bs((tm, D), lambda i, f: (i, 0)), _bs((tm, D), lambda i, f: (i, 0))],
        out_shape=[jax.ShapeDtypeStruct((M, D), f32), jax.ShapeDtypeStruct((M, D), bf16)],
        scratch_shapes=[pltpu.VMEM((tm, D), bf16), pltpu.VMEM((tm, D), f32)],
        compiler_params=_params(("parallel", "arbitrary")),
        name="ffn",
    )(x, g.reshape(1, D), wg, wu, wd, g_next.reshape(1, D))


def _mm_body(a_ref, w_ref, o_ref):
    o_ref[...] = _nn(a_ref[...], w_ref[...]).astype(o_ref.dtype)


def _mm(a, w, out_dtype, tm, tn, name):
    M, K = a.shape
    N = w.shape[1]
    return pl.pallas_call(
        _mm_body,
        grid=(M // tm, N // tn),
        in_specs=[_bs((tm, K), lambda i, j: (i, 0)), _bs((K, tn), lambda i, j: (0, j))],
        out_specs=_bs((tm, tn), lambda i, j: (i, j)),
        out_shape=jax.ShapeDtypeStruct((M, N), out_dtype),
        compiler_params=_params(("parallel", "parallel")),
        name=name,
    )(a, w)


KP = NSA_GROUPS * LANE


def _proj_kv_body(h_ref, wm_ref, wt_ref, kc_ref, vc_ref, ks_ref, vs_ref, kw_ref, vw_ref,
                  ksb_ref, kwb_ref, vst_ref, vwt_ref, gt_ref, kcb_ref, vcb_ref):
    h = h_ref[...]
    tm = h.shape[0]
    main = _nn(h, wm_ref[...])
    vw_ = NSA_GROUPS * NSA_DV
    o = 0
    kc = main[:, o:o + KP]; o += KP
    vc = main[:, o:o + vw_]; o += vw_
    ks = main[:, o:o + KP]; o += KP
    vs = main[:, o:o + vw_]; o += vw_
    kw = main[:, o:o + KP]; o += KP
    vw = main[:, o:o + vw_]
    kc_ref[...] = kc
    vc_ref[...] = vc
    ks_ref[...] = ks
    vs_ref[...] = vs
    kw_ref[...] = kw
    vw_ref[...] = vw
    ksb_ref[...] = ks.astype(bf16)
    kwb_ref[...] = kw.astype(bf16)
    tr = _nt(wt_ref[...], h)
    vst_ref[...] = tr[0:vw_].astype(bf16)
    vwt_ref[...] = tr[vw_:2 * vw_].astype(bf16)
    gt_ref[...] = tr[2 * vw_:]
    nb = tm // CMP_BLOCK
    kcb_ref[...] = kc.reshape(nb, CMP_BLOCK, KP).sum(axis=1) * (1.0 / CMP_BLOCK)
    vcb_ref[...] = vc.reshape(nb, CMP_BLOCK, vw_).sum(axis=1) * (1.0 / CMP_BLOCK)


def _proj_kv(h, wm, wt, tm):
    M, D = h.shape
    vw_ = NSA_GROUPS * NSA_DV
    ng = NSA_HEADS * 3
    nb = tm // CMP_BLOCK
    row = lambda w: _bs((tm, w), lambda i: (i, 0))
    col = lambda r: _bs((r, tm), lambda i: (0, i))
    sds = jax.ShapeDtypeStruct
    return pl.pallas_call(
        _proj_kv_body,
        grid=(M // tm,),
        in_specs=[_bs((tm, D), lambda i: (i, 0)), _bs(wm.shape, lambda i: (0, 0)), _bs(wt.shape, lambda i: (0, 0))],
        out_specs=[row(KP), row(vw_), row(KP), row(vw_), row(KP), row(vw_), row(KP), row(KP),
                   col(vw_), col(vw_), col(ng), _bs((nb, KP), lambda i: (i, 0)), _bs((nb, vw_), lambda i: (i, 0))],
        out_shape=[sds((M, KP), f32), sds((M, vw_), f32), sds((M, KP), f32), sds((M, vw_), f32),
                   sds((M, KP), f32), sds((M, vw_), f32), sds((M, KP), bf16), sds((M, KP), bf16),
                   sds((vw_, M), bf16), sds((vw_, M), bf16), sds((ng, M), f32),
                   sds((M // CMP_BLOCK, KP), f32), sds((M // CMP_BLOCK, vw_), f32)],
        compiler_params=_params(("parallel",)),
        name="proj_kv",
    )(h, wm, wt)


def _mla_proj_body(h_ref, wc_ref, gq_ref, gkv_ref, cos_ref, sin_ref, cqn_ref, ckv_ref, kr_ref, kcat_ref):
    m = _nn(h_ref[...], wc_ref[...])
    cqn_ref[...] = _rms(m[:, 0:Q_LORA], gq_ref[...]).astype(bf16)
    o = Q_LORA
    ckvn = _rms(m[:, o:o + KV_LORA], gkv_ref[...])
    o += KV_LORA
    kr = m[:, o:o + LANE] * cos_ref[...] + m[:, o + LANE:o + 2 * LANE] * sin_ref[...]
    ckv_ref[...] = ckvn
    kr_ref[...] = kr
    kcat_ref[:, 0:KV_LORA] = ckvn.astype(bf16)
    kcat_ref[:, KV_LORA:QCAT] = kr.astype(bf16)


def _mla_proj(h, wc, gq, gkv, cos, sin, tm):
    M, D = h.shape
    nt = cos.shape[0] // tm
    sds = jax.ShapeDtypeStruct
    return pl.pallas_call(
        _mla_proj_body,
        grid=(M // tm,),
        in_specs=[_bs((tm, D), lambda i: (i, 0)), _bs(wc.shape, lambda i: (0, 0)),
                  _bs((1, Q_LORA), lambda i: (0, 0)), _bs((1, KV_LORA), lambda i: (0, 0)),
                  _bs((tm, LANE), lambda i: (i % nt, 0)), _bs((tm, LANE), lambda i: (i % nt, 0))],
        out_specs=[_bs((tm, Q_LORA), lambda i: (i, 0)), _bs((tm, KV_LORA), lambda i: (i, 0)),
                   _bs((tm, LANE), lambda i: (i, 0)), _bs((tm, QCAT), lambda i: (i, 0))],
        out_shape=[sds((M, Q_LORA), bf16), sds((M, KV_LORA), f32), sds((M, LANE), f32), sds((M, QCAT), bf16)],
        compiler_params=_params(("parallel",)),
        name="mla_proj",
    )(h, wc, gq.reshape(1, -1), gkv.reshape(1, -1), cos, sin)


def _mla_q_body(c_ref, wn_ref, wa_ref, wb_ref, wkb_ref, cos_ref, sin_ref, o_ref):
    c = c_ref[...]
    qn = _nn(c, wn_ref[...])
    ra = _nn(c, wa_ref[...])
    rb = _nn(c, wb_ref[...])
    cos = cos_ref[...]
    sin = sin_ref[...]
    for h in range(MLA_HEADS):
        sl = slice(h * LANE, (h + 1) * LANE)
        o_ref[h, :, 0:KV_LORA] = _nn(qn[:, sl].astype(bf16), wkb_ref[h]).astype(bf16)
        o_ref[h, :, KV_LORA:QCAT] = (ra[:, sl] * cos + rb[:, sl] * sin).astype(bf16)


def _mla_q(cqn, wn, wa, wb, wkb, cos, sin, nb, tm):
    M = cqn.shape[0]
    T = M // nb
    nt = T // tm
    full = lambda a: _bs(a.shape, lambda n, i: (0,) * a.ndim)
    return pl.pallas_call(
        _mla_q_body,
        grid=(nb, nt),
        in_specs=[_bs((tm, Q_LORA), lambda n, i: (n * nt + i, 0)), full(wn), full(wa), full(wb), full(wkb),
                  _bs((tm, LANE), lambda n, i: (i, 0)), _bs((tm, LANE), lambda n, i: (i, 0))],
        out_specs=_bs((None, MLA_HEADS, tm, QCAT), lambda n, i: (n, 0, i, 0)),
        out_shape=jax.ShapeDtypeStruct((nb, MLA_HEADS, T, QCAT), bf16),
        compiler_params=_params(("parallel", "parallel")),
        name="mla_q",
    )(cqn, wn, wa, wb, wkb, cos, sin)


MLA_TS = 512


def _mla_attn_body(q_ref, k_ref, wvb_ref, o_ref, m_sc, l_sc, acc_sc):
    qb = pl.program_id(1)
    R = MLA_HEADS * Q_BLOCK
    q = q_ref[...].reshape(R, QCAT)
    scale = (QK_NOPE + QK_ROPE) ** -0.5
    m_sc[...] = jnp.full_like(m_sc, NEG)
    l_sc[...] = jnp.zeros_like(l_sc)
    acc_sc[...] = jnp.zeros_like(acc_sc)

    def tile(j, masked):
        k = k_ref[pl.ds(pl.multiple_of(j * MLA_TS, MLA_TS), MLA_TS), :]
        s = _nt(q, k) * scale
        if masked:
            t = qb * Q_BLOCK + (lax.broadcasted_iota(i32, s.shape, 0) & (Q_BLOCK - 1))
            c = j * MLA_TS + lax.broadcasted_iota(i32, s.shape, 1)
            s = jnp.where(c <= t, s, NEG)
        m_old = m_sc[...]
        m_new = jnp.maximum(m_old, jnp.max(s, axis=1, keepdims=True))
        alpha = jnp.exp(m_old - m_new)
        p = jnp.exp(s - m_new)
        l_sc[...] = alpha * l_sc[...] + jnp.sum(p, axis=1, keepdims=True)
        acc_sc[...] = alpha * acc_sc[...] + _nn(p.astype(bf16), k[:, 0:KV_LORA])
        m_sc[...] = m_new

    nfull = (qb * Q_BLOCK) // MLA_TS

    def loop_body(j, c):
        tile(j, False)
        return c

    lax.fori_loop(0, nfull, loop_body, 0)
    tile(nfull, True)
    o = (acc_sc[...] / l_sc[...]).astype(bf16)
    for h in range(MLA_HEADS):
        oh = o[h * Q_BLOCK:(h + 1) * Q_BLOCK, :]
        o_ref[:, h * V_DIM:(h + 1) * V_DIM] = _nn(oh, wvb_ref[h]).astype(bf16)


def _mla_attn(qcat, kcat, wvb):
    nb, H, T, _ = qcat.shape
    nq = T // Q_BLOCK
    R = H * Q_BLOCK
    return pl.pallas_call(
        _mla_attn_body,
        grid=(nb, nq),
        in_specs=[_bs((None, H, Q_BLOCK, QCAT), lambda n, i: (n, 0, i, 0)),
                  _bs((T, QCAT), lambda n, i: (n, 0)),
                  _bs(wvb.shape, lambda n, i: (0, 0, 0))],
        out_specs=_bs((Q_BLOCK, H * V_DIM), lambda n, i: (n * nq + i, 0)),
        out_shape=jax.ShapeDtypeStruct((nb * T, H * V_DIM), bf16),
        scratch_shapes=[pltpu.VMEM((R, 1), f32), pltpu.VMEM((R, 1), f32), pltpu.VMEM((R, KV_LORA), f32)],
        compiler_params=_params(("parallel", "arbitrary")),
        name="mla_attn",
    )(qcat, kcat, wvb)


NSA_KT = 128


def _flash_tile(lg, vt, m_sc, l_sc, acc_sc):
    m_old = m_sc[...]
    m_new = jnp.maximum(m_old, jnp.max(lg, axis=0, keepdims=True))
    alpha = jnp.exp(m_old - m_new)
    p = jnp.exp(lg - m_new)
    l_sc[...] = alpha * l_sc[...] + jnp.sum(p, axis=0, keepdims=True)
    acc_sc[...] = alpha * acc_sc[...] + _nn(vt, p.astype(bf16))
    m_sc[...] = m_new


def _tile_heads(x, n):
    return jnp.concatenate([x] * n, axis=1)


def _nsa_prompt_body(q_ref, kcb_ref, vcbt_ref, cb_ref, ks_ref, vst_ref, kw_ref, vwt_ref, g_ref,
                     tz0_ref, tz1_ref, b31_ref, o_ref,
                     imp_sc, sel_sc, m1, l1, a1, m2, l2, a2):
    qb = pl.program_id(2)
    HP, QB = NSA_HPG, Q_BLOCK
    C = HP * QB
    scale = NSA_DK ** -0.5
    q2 = q_ref[...]
    q = jnp.concatenate([q2[:, h * LANE:(h + 1) * LANE] for h in range(HP)], axis=0)

    cb = jnp.concatenate([cb_ref[h] for h in range(HP)], axis=1)
    lc = _nt(kcb_ref[...].astype(bf16), q) * scale + cb
    mc = jnp.max(lc, axis=0, keepdims=True)
    ec = jnp.where(cb > 0.5 * NEG, jnp.exp(lc - mc), 0.0)
    pc = ec / jnp.maximum(jnp.sum(ec, axis=0, keepdims=True), 1e-30)
    o_cmp = _nn(vcbt_ref[...].astype(bf16), pc.astype(bf16))
    imp = pc[:, 0:QB]
    for h in range(1, HP):
        imp = imp + pc[:, h * QB:(h + 1) * QB]
    imp_sc[...] = imp
    n_cmp = imp.shape[0]
    n_sel = n_cmp // 2
    imp2 = imp_sc[pl.ds(0, n_sel, stride=2), :] + imp_sc[pl.ds(1, n_sel, stride=2), :]

    blk = lax.broadcasted_iota(i32, (n_sel, QB), 0)
    tpos = qb * QB + lax.broadcasted_iota(i32, (n_sel, QB), 1)
    cur = tpos // SEL_BLOCK
    valid = blk <= cur
    forced = (blk == 0) | (blk == cur) | (blk == cur - 1)
    score = jnp.where(valid, jnp.where(forced, BIG, imp2), NEG)
    rank = jnp.zeros((n_sel, QB), i32)
    for i in range(n_sel):
        r = score[i:i + 1, :]
        rank = rank + jnp.where(blk > i, jnp.where(r >= score, 1, 0), jnp.where(r > score, 1, 0))
    chosen = (rank < TOP_N) & valid
    sel_sc[...] = jnp.where(chosen, 0.0, NEG)

    b31 = b31_ref[...]

    def reset(m, l, a):
        m[...] = jnp.full_like(m, NEG)
        l[...] = jnp.zeros_like(l)
        a[...] = jnp.zeros_like(a)

    def logits(k_ref_, kt):
        k = k_ref_[pl.ds(pl.multiple_of(kt * NSA_KT, NSA_KT), NSA_KT), :]
        return _nt(k, q) * scale

    def vtile(vt_ref_, kt):
        return vt_ref_[:, pl.ds(pl.multiple_of(kt * NSA_KT, NSA_KT), NSA_KT)]

    def sel_mask(kt):
        nblk = NSA_KT // SEL_BLOCK
        rows = [jnp.broadcast_to(sel_sc[pl.ds(kt * nblk + b, 1), :], (SEL_BLOCK, QB)) for b in range(nblk)]
        return _tile_heads(jnp.concatenate(rows, axis=0), HP)

    def toeplitz(tz_ref):
        return jnp.concatenate([tz_ref[h] for h in range(HP)], axis=1)

    reset(m1, l1, a1)

    def sel_far(kt, c):
        _flash_tile(logits(ks_ref, kt) + b31 + sel_mask(kt), vtile(vst_ref, kt), m1, l1, a1)
        return c

    lax.fori_loop(0, jnp.maximum(qb - 1, 0), sel_far, 0)

    @pl.when(qb >= 1)
    def _():
        _flash_tile(logits(ks_ref, qb - 1) + toeplitz(tz1_ref) + sel_mask(qb - 1), vtile(vst_ref, qb - 1), m1, l1, a1)

    _flash_tile(logits(ks_ref, qb) + toeplitz(tz0_ref) + sel_mask(qb), vtile(vst_ref, qb), m1, l1, a1)

    reset(m2, l2, a2)
    nwt = WINDOW // NSA_KT

    @pl.when(qb >= nwt)
    def _():
        kt = qb - nwt
        srow = lax.broadcasted_iota(i32, (NSA_KT, C), 0)
        tcol = lax.broadcasted_iota(i32, (NSA_KT, C), 1) & (QB - 1)
        edge = jnp.where(srow > tcol, 0.0, NEG)
        _flash_tile(logits(kw_ref, kt) + b31 + edge, vtile(vwt_ref, kt), m2, l2, a2)

    for d in range(nwt - 1, 1, -1):
        @pl.when(qb >= d)
        def _(d=d):
            _flash_tile(logits(kw_ref, qb - d) + b31, vtile(vwt_ref, qb - d), m2, l2, a2)

    @pl.when(qb >= 1)
    def _():
        _flash_tile(logits(kw_ref, qb - 1) + toeplitz(tz1_ref), vtile(vwt_ref, qb - 1), m2, l2, a2)

    _flash_tile(logits(kw_ref, qb) + toeplitz(tz0_ref), vtile(vwt_ref, qb), m2, l2, a2)

    o_sel = a1[...] / l1[...]
    o_win = a2[...] / l2[...]
    sg = jax.nn.sigmoid(g_ref[...])
    parts = []
    for h in range(HP):
        sl = slice(h * QB, (h + 1) * QB)
        parts.append(sg[h:h + 1] * o_cmp[:, sl] + sg[HP + h:HP + h + 1] * o_sel[:, sl]
                     + sg[2 * HP + h:2 * HP + h + 1] * o_win[:, sl])
    o_ref[...] = jnp.concatenate(parts, axis=0).T.astype(bf16)


def _nsa_prompt(q_pad, kcb, vcbt, cbias, ks_b, vst, kw_b, vwt, gates_t, tz0, tz1, b31, nb):
    M = q_pad.shape[0]
    T = M // nb
    nq = T // Q_BLOCK
    n_cmp = T // CMP_BLOCK
    G, HP, DV = NSA_GROUPS, NSA_HPG, NSA_DV
    C = HP * Q_BLOCK
    return pl.pallas_call(
        _nsa_prompt_body,
        grid=(nb, G, nq),
        in_specs=[
            _bs((Q_BLOCK, HP * LANE), lambda n, g, i: (n * nq + i, g)),
            _bs((n_cmp, LANE), lambda n, g, i: (n, g)),
            _bs((None, None, DV, n_cmp), lambda n, g, i: (n, g, 0, 0)),
            _bs((HP, n_cmp, Q_BLOCK), lambda n, g, i: (g, 0, i)),
            _bs((T, LANE), lambda n, g, i: (n, g)),
            _bs((DV, T), lambda n, g, i: (g, n)),
            _bs((T, LANE), lambda n, g, i: (n, g)),
            _bs((DV, T), lambda n, g, i: (g, n)),
            _bs((None, 3 * HP, Q_BLOCK), lambda n, g, i: (g, 0, n * nq + i)),
            _bs((HP, NSA_KT, Q_BLOCK), lambda n, g, i: (g, 0, 0)),
            _bs((HP, NSA_KT, Q_BLOCK), lambda n, g, i: (g, 0, 0)),
            _bs((None, 1, C), lambda n, g, i: (g, 0, 0)),
        ],
        out_specs=_bs((Q_BLOCK, HP * DV), lambda n, g, i: (n * nq + i, g)),
        out_shape=jax.ShapeDtypeStruct((M, G * HP * DV), bf16),
        scratch_shapes=[pltpu.VMEM((n_cmp, Q_BLOCK), f32), pltpu.VMEM((n_cmp // 2, Q_BLOCK), f32),
                        pltpu.VMEM((1, C), f32), pltpu.VMEM((1, C), f32), pltpu.VMEM((DV, C), f32),
                        pltpu.VMEM((1, C), f32), pltpu.VMEM((1, C), f32), pltpu.VMEM((DV, C), f32)],
        compiler_params=_params(("parallel", "parallel", "arbitrary")),
        name="nsa_prompt",
    )(q_pad, kcb, vcbt, cbias, ks_b, vst, kw_b, vwt, gates_t, tz0, tz1, b31)


def _merge_body(x_ref, h_ref, on_ref, v_ref, wga_ref, wgb_ref, wno_ref, wmo_ref, wo_ref, o_ref, acc_sc):
    j = pl.program_id(1)

    @pl.when(j == 0)
    def _():
        acc_sc[...] = jnp.zeros_like(acc_sc)

    h = h_ref[...]
    ga = jax.nn.sigmoid(_nn(h, wga_ref[...]))
    gb = jax.nn.sigmoid(_nn(h, wgb_ref[...]))
    mix = ga * _nn(on_ref[...], wno_ref[...]) + gb * _nn(v_ref[...], wmo_ref[...])
    acc_sc[...] += _nn(mix.astype(bf16), wo_ref[...])

    @pl.when(j == pl.num_programs(1) - 1)
    def _():
        o_ref[...] = x_ref[...] + acc_sc[...]


def _merge(x, h, o_nsa, v, wga, wgb, wno, wmo, wo, tm):
    M, D = x.shape
    tn = min(256, D)
    Kn, Kv = o_nsa.shape[1], v.shape[1]
    return pl.pallas_call(
        _merge_body,
        grid=(M // tm, D // tn),
        in_specs=[_bs((tm, D), lambda i, j: (i, 0)), _bs((tm, D), lambda i, j: (i, 0)),
                  _bs((tm, Kn), lambda i, j: (i, 0)), _bs((tm, Kv), lambda i, j: (i, 0)),
                  _bs((D, tn), lambda i, j: (0, j)), _bs((D, tn), lambda i, j: (0, j)),
                  _bs((Kn, tn), lambda i, j: (0, j)), _bs((Kv, tn), lambda i, j: (0, j)),
                  _bs((tn, D), lambda i, j: (j, 0))],
        out_specs=_bs((tm, D), lambda i, j: (i, 0)),
        out_shape=jax.ShapeDtypeStruct((M, D), f32),
        scratch_shapes=[pltpu.VMEM((tm, D), f32)],
        compiler_params=_params(("parallel", "arbitrary")),
        name="merge",
    )(x, h, o_nsa, v, wga, wgb, wno, wmo, wo)


def _ple_body(x_ref, h_ref, p_ref, wg_ref, wp_ref, gf_ref, o_ref):
    gate = jax.nn.sigmoid(_nn(h_ref[...], wg_ref[...]))
    y = x_ref[...] + gate * _nn(p_ref[...].astype(bf16), wp_ref[...])
    o_ref[...] = _rms(y, gf_ref[...])


def _ple(x, h, p, wg, wp, gf, tm):
    M, D = x.shape
    P = p.shape[1]
    return pl.pallas_call(
        _ple_body,
        grid=(M // tm,),
        in_specs=[_bs((tm, D), lambda i: (i, 0)), _bs((tm, D), lambda i: (i, 0)), _bs((tm, P), lambda i: (i, 0)),
                  _bs((D, D), lambda i: (0, 0)), _bs((P, D), lambda i: (0, 0)), _bs((1, D), lambda i: (0, 0))],
        out_specs=_bs((tm, D), lambda i: (i, 0)),
        out_shape=jax.ShapeDtypeStruct((M, D), f32),
        compiler_params=_params(("parallel",)),
        name="ple",
    )(x, h, p, wg, wp, gf.reshape(1, D))


PAGE_CHUNK = 8


def _mla_sample_body(pt_ref, q_ref, kn_ref, ckv_hbm, kr_hbm, o_ref, cbuf, rbuf, sem, rz, m_sc, l_sc, acc_sc):
    n = pl.program_id(0)
    ns = pl.num_programs(0)
    n_pages = pt_ref.shape[1]
    nch = n_pages // PAGE_CHUNK
    scale = (QK_NOPE + QK_ROPE) ** -0.5

    def copies(seq, ch, slot):
        out = []
        for i in range(PAGE_CHUNK):
            phys = pt_ref[seq, ch * PAGE_CHUNK + i]
            out.append(pltpu.make_async_copy(ckv_hbm.at[phys], cbuf.at[slot, i], sem.at[0, slot]))
            out.append(pltpu.make_async_copy(kr_hbm.at[phys], rbuf.at[slot, i], sem.at[1, slot]))
        return out

    @pl.when(n == 0)
    def _():
        for c in copies(0, 0, 0):
            c.start()

    @pl.when(n == 0)
    def _():
        rz[...] = jnp.zeros_like(rz)

    q = q_ref[...]
    ql = q[:, 0:KV_LORA]
    qr = q[:, KV_LORA:QCAT]
    kn = kn_ref[...].astype(f32)
    m_sc[...] = jnp.sum(q.astype(f32) * kn, axis=1, keepdims=True) * scale
    l_sc[...] = jnp.ones_like(l_sc)
    acc_sc[...] = jnp.broadcast_to(kn[:, 0:KV_LORA], acc_sc.shape)

    def chunk(ch, carry):
        g = n * nch + ch
        slot = g % 2
        nxt = g + 1

        @pl.when(nxt < ns * nch)
        def _():
            for c in copies(nxt // nch, nxt % nch, 1 - slot):
                c.start()

        for c in copies(n, ch, slot):
            c.wait()
        kc = cbuf[slot].reshape(PAGE_CHUNK * PAGE_SIZE, KV_LORA).astype(bf16)
        rz[:, 0:QK_ROPE] = rbuf[slot].reshape(PAGE_CHUNK * PAGE_SIZE, QK_ROPE)
        s = (_nt(ql, kc) + _nt(qr, rz[...].astype(bf16))) * scale
        m_old = m_sc[...]
        m_new = jnp.maximum(m_old, jnp.max(s, axis=1, keepdims=True))
        alpha = jnp.exp(m_old - m_new)
        p = jnp.exp(s - m_new)
        l_sc[...] = alpha * l_sc[...] + jnp.sum(p, axis=1, keepdims=True)
        acc_sc[...] = alpha * acc_sc[...] + _nn(p.astype(bf16), kc)
        m_sc[...] = m_new
        return carry

    lax.fori_loop(0, nch, chunk, 0)
    o_ref[...] = (acc_sc[...] / l_sc[...]).astype(bf16)


def _mla_sample(page_table, q_s, kcat_s, ckv_cache, kr_cache):
    ns, H, _ = q_s.shape
    gs = pltpu.PrefetchScalarGridSpec(
        num_scalar_prefetch=1,
        grid=(ns,),
        in_specs=[_bs((None, H, QCAT), lambda n, pt: (n, 0, 0)),
                  _bs((None, 1, QCAT), lambda n, pt: (n, 0, 0)),
                  pl.BlockSpec(memory_space=pl.ANY), pl.BlockSpec(memory_space=pl.ANY)],
        out_specs=_bs((None, H, KV_LORA), lambda n, pt: (n, 0, 0)),
        scratch_shapes=[pltpu.VMEM((2, PAGE_CHUNK, PAGE_SIZE, KV_LORA), f32),
                        pltpu.VMEM((2, PAGE_CHUNK, PAGE_SIZE, QK_ROPE), f32),
                        pltpu.SemaphoreType.DMA((2, 2)),
                        pltpu.VMEM((PAGE_CHUNK * PAGE_SIZE, LANE), f32),
                        pltpu.VMEM((H, 1), f32), pltpu.VMEM((H, 1), f32), pltpu.VMEM((H, KV_LORA), f32)],
    )
    return pl.pallas_call(
        _mla_sample_body,
        grid_spec=gs,
        out_shape=jax.ShapeDtypeStruct((ns, H, KV_LORA), bf16),
        compiler_params=_params(("arbitrary",)),
        name="mla_sample",
    )(page_table, q_s, kcat_s.reshape(ns, 1, QCAT), ckv_cache, kr_cache)


def _vb_body(o_ref, w_ref, v_ref):
    v_ref[...] = _nn(o_ref[...], w_ref[...]).astype(bf16)


def _vb_sample(o_t, wvb):
    H, ns, _ = o_t.shape
    return pl.pallas_call(
        _vb_body,
        grid=(H,),
        in_specs=[_bs((None, ns, KV_LORA), lambda h: (h, 0, 0)), _bs((None, KV_LORA, V_DIM), lambda h: (h, 0, 0))],
        out_specs=_bs((ns, V_DIM), lambda h: (0, h)),
        out_shape=jax.ShapeDtypeStruct((ns, H * V_DIM), bf16),
        compiler_params=_params(("parallel",)),
        name="vb_sample",
    )(o_t, wvb)


def _cmp_sample_body(pt_ref, q_ref, cbe_ref, cbo_ref, ck_hbm, cv_hbm, o_ref, imp_ref,
                     kbuf, vbuf, sem, kmean, vmean):
    n = pl.program_id(0)
    ns = pl.num_programs(0)
    n_pages = pt_ref.shape[1]
    nch = n_pages // PAGE_CHUNK
    G, HP = NSA_GROUPS, NSA_HPG
    kw_, vw_ = G * NSA_DK, G * NSA_DV
    bpp = PAGE_SIZE // CMP_BLOCK
    rows = PAGE_CHUNK * bpp
    scale = NSA_DK ** -0.5

    def copies(seq, ch, slot):
        out = []
        for i in range(PAGE_CHUNK):
            phys = pt_ref[seq, ch * PAGE_CHUNK + i]
            out.append(pltpu.make_async_copy(ck_hbm.at[phys], kbuf.at[slot, i], sem.at[0, slot]))
            out.append(pltpu.make_async_copy(cv_hbm.at[phys], vbuf.at[slot, i], sem.at[1, slot]))
        return out

    @pl.when(n == 0)
    def _():
        for c in copies(0, 0, 0):
            c.start()
        kmean[...] = jnp.zeros_like(kmean)

    def chunk(ch, carry):
        g = n * nch + ch
        slot = g % 2
        nxt = g + 1

        @pl.when(nxt < ns * nch)
        def _():
            for c in copies(nxt // nch, nxt % nch, 1 - slot):
                c.start()

        for c in copies(n, ch, slot):
            c.wait()
        km = kbuf[slot].reshape(rows, CMP_BLOCK, kw_).sum(axis=1) * (1.0 / CMP_BLOCK)
        vm = vbuf[slot].reshape(rows, CMP_BLOCK, vw_).sum(axis=1) * (1.0 / CMP_BLOCK)
        r0 = pl.multiple_of(ch * rows, rows)
        kmean[0, pl.ds(r0, rows), :] = km[:, 0:LANE]
        kmean[1, pl.ds(r0, rows), 0:kw_ - LANE] = km[:, LANE:kw_]
        vmean[pl.ds(r0, rows), :] = vm
        return carry

    lax.fori_loop(0, nch, chunk, 0)

    n_cmp = n_pages * bpp
    half = n_cmp // 2
    stride2 = lambda par: jnp.concatenate([kmean[0, pl.ds(par, half, stride=2), :],
                                           kmean[1, pl.ds(par, half, stride=2), :]], axis=1).astype(bf16)
    ke = stride2(0)
    ko = stride2(1)
    ve = vmean[pl.ds(0, half, stride=2), :].astype(bf16)
    vo = vmean[pl.ds(1, half, stride=2), :].astype(bf16)
    qrow = q_ref[...]
    for g in range(G):
        qg = jnp.concatenate([qrow[:, (g * HP + h) * 2 * LANE:(g * HP + h + 1) * 2 * LANE] for h in range(HP)], axis=0)
        le = _nt(qg, ke) * scale + cbe_ref[g * HP:(g + 1) * HP, :]
        lo = _nt(qg, ko) * scale + cbo_ref[g * HP:(g + 1) * HP, :]
        m = jnp.maximum(jnp.max(le, axis=1, keepdims=True), jnp.max(lo, axis=1, keepdims=True))
        ee = jnp.exp(le - m)
        eo = jnp.exp(lo - m)
        den = jnp.maximum(jnp.sum(ee, axis=1, keepdims=True) + jnp.sum(eo, axis=1, keepdims=True), 1e-30)
        pe = ee / den
        po = eo / den
        o_ref[g * HP:(g + 1) * HP, :] = _nn(pe.astype(bf16), ve) + _nn(po.astype(bf16), vo)
        imp_ref[g:g + 1, :] = jnp.sum(pe + po, axis=0, keepdims=True)


def _cmp_sample(page_table, q_s2, cbe, cbo, ck_cache, cv_cache):
    ns, n_pages = page_table.shape
    G, HP = NSA_GROUPS, NSA_HPG
    kw_, vw_ = G * NSA_DK, G * NSA_DV
    n_cmp = n_pages * (PAGE_SIZE // CMP_BLOCK)
    half = n_cmp // 2
    gs = pltpu.PrefetchScalarGridSpec(
        num_scalar_prefetch=1,
        grid=(ns,),
        in_specs=[_bs((None, 1, q_s2.shape[-1]), lambda n, pt: (n, 0, 0)),
                  _bs(cbe.shape, lambda n, pt: (0, 0)), _bs(cbo.shape, lambda n, pt: (0, 0)),
                  pl.BlockSpec(memory_space=pl.ANY), pl.BlockSpec(memory_space=pl.ANY)],
        out_specs=[_bs((None, G * HP, vw_), lambda n, pt: (n, 0, 0)), _bs((None, G, half), lambda n, pt: (n, 0, 0))],
        scratch_shapes=[pltpu.VMEM((2, PAGE_CHUNK, PAGE_SIZE, kw_), f32),
                        pltpu.VMEM((2, PAGE_CHUNK, PAGE_SIZE, vw_), f32),
                        pltpu.SemaphoreType.DMA((2, 2)),
                        pltpu.VMEM((2, n_cmp, LANE), f32), pltpu.VMEM((n_cmp, vw_), f32)],
    )
    return pl.pallas_call(
        _cmp_sample_body,
        grid_spec=gs,
        out_shape=[jax.ShapeDtypeStruct((ns, G * HP, vw_), f32), jax.ShapeDtypeStruct((ns, G, half), f32)],
        compiler_params=_params(("arbitrary",)),
        name="cmp_sample",
    )(page_table, q_s2.reshape(ns, 1, -1), cbe, cbo, ck_cache, cv_cache)


def _select_sample_body(imp_ref, idx_ref, score_sc):
    g = pl.program_id(0)
    s = imp_ref[...].T
    nblk, ns = s.shape
    blk = lax.broadcasted_iota(i32, (nblk, ns), 0)
    forced = (blk == 0) | (blk == nblk - 1)
    score_sc[...] = jnp.where(forced, BIG, s)

    def body(i, rank):
        score = score_sc[...]
        r = score_sc[pl.ds(i, 1), :]
        return rank + jnp.where(blk > i, jnp.where(r >= score, 1, 0), jnp.where(r > score, 1, 0))

    rank = lax.fori_loop(0, nblk, body, jnp.zeros((nblk, ns), i32))
    for r_ in range(TOP_N - 1):
        idx_ref[r_:r_ + 1, :] = jnp.sum(jnp.where(rank == r_, blk, 0), axis=0, keepdims=True)
    idx_ref[TOP_N - 1:TOP_N, :] = jnp.full((1, ns), nblk, i32)


def _select_sample(imp_t):
    G, ns, nblk = imp_t.shape
    return pl.pallas_call(
        _select_sample_body,
        grid=(G,),
        in_specs=[_bs((None, ns, nblk), lambda g: (g, 0, 0))],
        out_specs=_bs((None, TOP_N, ns), lambda g: (g, 0, 0)),
        out_shape=jax.ShapeDtypeStruct((G, TOP_N, ns), i32),
        scratch_shapes=[pltpu.VMEM((nblk, ns), f32)],
        compiler_params=_params(("parallel",)),
        name="select_sample",
    )(imp_t)


def _selwin_sample_body(pt_ref, idx_ref, q_ref, ksn_ref, vsn_ref, kwn_ref, vwn_ref, wk_ref, wv_ref,
                        b31_ref, b0_ref, tbl_ref, tbw_ref, sk_hbm, sv_hbm, osel_ref, owin_ref,
                        gk, gv, tk, tv, sem, kz, wz):
    n = pl.program_id(0)
    ns = pl.num_programs(0)
    G, HP = NSA_GROUPS, NSA_HPG
    kw_, vw_ = G * NSA_DK, G * NSA_DV
    n_pages = pt_ref.shape[1]
    bpp = PAGE_SIZE // SEL_BLOCK
    nblk = n_pages * bpp
    NG = TOP_N - 1
    scale = NSA_DK ** -0.5

    def copies(seq, slot):
        out = []
        for g in range(G):
            for k in range(NG):
                b = jnp.minimum(idx_ref[seq, g * TOP_N + k], nblk - 1)
                phys = pt_ref[seq, b // bpp]
                off = pl.multiple_of((b % bpp) * SEL_BLOCK, SEL_BLOCK)
                out.append(pltpu.make_async_copy(sk_hbm.at[phys, pl.ds(off, SEL_BLOCK)],
                                                 gk.at[slot, g, pl.ds(k * SEL_BLOCK, SEL_BLOCK)], sem.at[0, slot]))
                out.append(pltpu.make_async_copy(sv_hbm.at[phys, pl.ds(off, SEL_BLOCK)],
                                                 gv.at[slot, g, pl.ds(k * SEL_BLOCK, SEL_BLOCK)], sem.at[1, slot]))
        last = pt_ref[seq, n_pages - 1]
        out.append(pltpu.make_async_copy(sk_hbm.at[last], tk.at[slot], sem.at[2, slot]))
        out.append(pltpu.make_async_copy(sv_hbm.at[last], tv.at[slot], sem.at[3, slot]))
        return out

    @pl.when(n == 0)
    def _():
        kz[...] = jnp.zeros_like(kz)
        wz[...] = jnp.zeros_like(wz)
        gv[...] = jnp.zeros_like(gv)
        gk[...] = jnp.zeros_like(gk)
        for c in copies(0, 0):
            c.start()

    slot = n % 2

    @pl.when(n + 1 < ns)
    def _():
        for c in copies(n + 1, 1 - slot):
            c.start()

    for c in copies(n, slot):
        c.wait()

    qrow = q_ref[...]
    ngk = TOP_N * SEL_BLOCK
    lane = lax.broadcasted_iota(i32, (1, ngk), 1)
    tl = lax.broadcasted_iota(i32, (1, PAGE_SIZE), 1)
    wz[:, 0:kw_] = wk_ref[...]
    wkb = wz[...].astype(bf16)
    wvb = wv_ref[...].astype(bf16)
    for g in range(G):
        qg = jnp.concatenate([qrow[:, (g * HP + h) * 2 * LANE:(g * HP + h + 1) * 2 * LANE] for h in range(HP)], axis=0)
        qf = qg.astype(f32)
        hs = slice(g * HP, (g + 1) * HP)
        kz[0:ngk, 0:kw_] = gk[slot, g]
        kz[ngk:ngk + PAGE_SIZE, 0:kw_] = tk[slot]
        kall = kz[...].astype(bf16)
        lg = _nt(qg, kall) * scale
        gmask = jnp.full((1, ngk), NEG, f32)
        in_tail = [jnp.zeros((), i32)] * bpp
        has_new = jnp.zeros((), i32)
        for k in range(TOP_N):
            b = idx_ref[n, g * TOP_N + k]
            has_new = has_new + (b >= nblk).astype(i32)
            if k < NG:
                gmask = jnp.where(lane // SEL_BLOCK == k, jnp.where(b < nblk - bpp, 0.0, NEG), gmask)
            for j in range(bpp):
                in_tail[j] = in_tail[j] + (b == nblk - bpp + j).astype(i32)
        tmask = jnp.full((1, PAGE_SIZE), NEG, f32)
        for j in range(bpp):
            tmask = jnp.where(tl // SEL_BLOCK == j, jnp.where(in_tail[j] > 0, 0.0, NEG), tmask)
        lgat = lg[:, 0:ngk] + b31_ref[hs, :][:, 0:1] + gmask
        ltail = lg[:, ngk:] + tbl_ref[hs, :] + tmask
        ksn = ksn_ref[...].astype(bf16).astype(f32)
        lnew = jnp.sum(qf * ksn, axis=1, keepdims=True) * scale + b0_ref[hs, :][:, 0:1]
        lnew = lnew + jnp.where(has_new > 0, 0.0, NEG)
        m = jnp.maximum(jnp.maximum(jnp.max(lgat, axis=1, keepdims=True), jnp.max(ltail, axis=1, keepdims=True)), lnew)
        eg = jnp.exp(lgat - m)
        et = jnp.exp(ltail - m)
        en = jnp.exp(lnew - m)
        den = jnp.sum(eg, axis=1, keepdims=True) + jnp.sum(et, axis=1, keepdims=True) + en
        vnew = vsn_ref[...].astype(bf16).astype(f32)
        num = (_nn(eg.astype(bf16), gv[slot, g].astype(bf16)) + _nn(et.astype(bf16), tv[slot].astype(bf16))
               + en.astype(bf16).astype(f32) * vnew)
        osel_ref[hs, :] = num / den
        lw = _nt(qg, wkb) * scale + tbw_ref[hs, :]
        kwn = kwn_ref[...].astype(bf16).astype(f32)
        lwn = jnp.sum(qf * kwn, axis=1, keepdims=True) * scale + b0_ref[hs, :][:, 0:1]
        mw = jnp.maximum(jnp.max(lw, axis=1, keepdims=True), lwn)
        ew = jnp.exp(lw - mw)
        ewn = jnp.exp(lwn - mw)
        denw = jnp.sum(ew, axis=1, keepdims=True) + ewn
        vwn = vwn_ref[...].astype(bf16).astype(f32)
        owin_ref[hs, :] = (_nn(ew.astype(bf16), wvb) + ewn.astype(bf16).astype(f32) * vwn) / denw


def _selwin_sample(page_table, idx, q_s2, ksn, vsn, kwn, vwn, win_k, win_v, b31, b0, tbl, tbw, sk_cache, sv_cache):
    ns, n_pages = page_table.shape
    G, HP = NSA_GROUPS, NSA_HPG
    kw_, vw_ = G * NSA_DK, G * NSA_DV
    wlen = win_k.shape[1]
    ngk = TOP_N * SEL_BLOCK
    rowspec = lambda w: _bs((None, 1, w), lambda n, pt, ix: (n, 0, 0))
    full2 = lambda a: _bs(a.shape, lambda n, pt, ix: (0, 0))
    gs = pltpu.PrefetchScalarGridSpec(
        num_scalar_prefetch=2,
        grid=(ns,),
        in_specs=[rowspec(q_s2.shape[-1]), rowspec(2 * LANE), rowspec(vw_), rowspec(2 * LANE), rowspec(vw_),
                  _bs((None, wlen, kw_), lambda n, pt, ix: (n, 0, 0)), _bs((None, wlen, vw_), lambda n, pt, ix: (n, 0, 0)),
                  full2(b31), full2(b0), full2(tbl), full2(tbw),
                  pl.BlockSpec(memory_space=pl.ANY), pl.BlockSpec(memory_space=pl.ANY)],
        out_specs=[_bs((None, G * HP, vw_), lambda n, pt, ix: (n, 0, 0)),
                   _bs((None, G * HP, vw_), lambda n, pt, ix: (n, 0, 0))],
        scratch_shapes=[pltpu.VMEM((2, G, ngk, kw_), f32), pltpu.VMEM((2, G, ngk, vw_), f32),
                        pltpu.VMEM((2, PAGE_SIZE, kw_), f32), pltpu.VMEM((2, PAGE_SIZE, vw_), f32),
                        pltpu.SemaphoreType.DMA((4, 2)),
                        pltpu.VMEM((ngk + PAGE_SIZE, 2 * LANE), f32), pltpu.VMEM((wlen, 2 * LANE), f32)],
    )
    r3 = lambda a: a.reshape(ns, 1, -1)
    return pl.pallas_call(
        _selwin_sample_body,
        grid_spec=gs,
        out_shape=[jax.ShapeDtypeStruct((ns, G * HP, vw_), f32), jax.ShapeDtypeStruct((ns, G * HP, vw_), f32)],
        compiler_params=_params(("arbitrary",)),
        name="selwin_sample",
    )(page_table, idx, r3(q_s2), r3(ksn), r3(vsn), r3(kwn), r3(vwn), win_k, win_v, b31, b0, tbl, tbw,
      sk_cache, sv_cache)


def _combine_sample_body(h_ref, wg_ref, oc_ref, os_ref, ow_ref, o_ref):
    h = h_ref[...]
    acc = jax.nn.sigmoid(_nn(h, wg_ref[0])) * oc_ref[...]
    acc = acc + jax.nn.sigmoid(_nn(h, wg_ref[1])) * os_ref[...]
    acc = acc + jax.nn.sigmoid(_nn(h, wg_ref[2])) * ow_ref[...]
    o_ref[...] = acc.astype(bf16)


def _combine_sample(h, wg3, oc, os_, ow):
    M, D = h.shape
    W = oc.shape[1]
    tn = 512
    return pl.pallas_call(
        _combine_sample_body,
        grid=(W // tn,),
        in_specs=[_bs((M, D), lambda j: (0, 0)), _bs((3, D, tn), lambda j: (0, 0, j)),
                  _bs((M, tn), lambda j: (0, j)), _bs((M, tn), lambda j: (0, j)), _bs((M, tn), lambda j: (0, j))],
        out_specs=_bs((M, tn), lambda j: (0, j)),
        out_shape=jax.ShapeDtypeStruct((M, W), bf16),
        compiler_params=_params(("parallel",)),
        name="combine_sample",
    )(h, wg3, oc, os_, ow)


def _bucket(rel):
    n = jnp.maximum(rel, 0)
    max_exact = NUM_BUCKETS // 2
    nf = jnp.maximum(n, 1).astype(f32)
    large = max_exact + (jnp.log(nf / max_exact) / math.log(MAX_DISTANCE / max_exact)
                         * (NUM_BUCKETS - max_exact)).astype(i32)
    return jnp.where(n < max_exact, n, jnp.minimum(large, NUM_BUCKETS - 1))


def _rel_bias(table, rel):
    return jnp.moveaxis(table[_bucket(rel)], -1, 0).astype(f32)


def _rope_tables(pos):
    half = QK_ROPE // 2
    inv_freq = ROPE_THETA ** (-jnp.arange(half, dtype=f32) / half)
    ang = pos.astype(f32)[:, None] * inv_freq[None, :]
    cos, sin = jnp.cos(ang), jnp.sin(ang)
    z = jnp.zeros((pos.shape[0], LANE - QK_ROPE), f32)
    return jnp.concatenate([cos, cos, z], axis=1), jnp.concatenate([-sin, sin, z], axis=1)


def _pad_cols(w, n, d, dp, off=0):
    K = w.shape[0]
    w3 = w.reshape(K, n, d)
    return jnp.pad(w3, ((0, 0), (0, 0), (off, dp - d - off))).reshape(K, n * dp)


def _prep_weights(w_in, w_qb, w_kb, w_vb, w_nsa_out):
    D = w_in.shape[0]
    G, HP, DK, DV = NSA_GROUPS, NSA_HPG, NSA_DK, NSA_DV
    splits = np.cumsum(_col_sizes())[:-1]
    wq, wkc, wvc, wks, wvs, wkw, wvw, wgn, wcq, wckv, wkr, wgm = jnp.split(w_in, splits, axis=1)
    W = {}
    W["q"] = _pad_cols(wq, NSA_HEADS, DK, LANE).astype(bf16)
    wq4 = wq.reshape(D, G, HP, DK)
    slots = []
    for g in range(G):
        slots.append(jnp.pad(wq4[:, g], ((0, 0), (0, 0), (g * DK, 2 * LANE - DK - g * DK))))
    W["q_slots"] = jnp.stack(slots, axis=1).reshape(D, G * HP * 2 * LANE).astype(bf16)
    pk = lambda w: _pad_cols(w, G, DK, LANE)
    W["kv_main"] = jnp.concatenate([pk(wkc), wvc, pk(wks), wvs, pk(wkw), wvw], axis=1).astype(bf16)
    wg_r = wgn.reshape(D, G, HP, 3).transpose(0, 1, 3, 2).reshape(D, G * 3 * HP)
    W["kv_t"] = jnp.concatenate([wvs, wvw, wg_r], axis=1).T.astype(bf16)
    wg4 = wgn.reshape(D, NSA_HEADS, 3)
    W["gate_rep"] = jnp.stack([jnp.repeat(wg4[:, :, b], LANE, axis=1) for b in range(3)], axis=0).astype(bf16)
    kr_a = jnp.pad(wkr, ((0, 0), (0, LANE - QK_ROPE)))
    half = QK_ROPE // 2
    kr_b = jnp.pad(jnp.concatenate([wkr[:, half:], wkr[:, :half]], axis=1), ((0, 0), (0, LANE - QK_ROPE)))
    W["mla_c"] = jnp.concatenate([wcq, wckv, kr_a, kr_b], axis=1).astype(bf16)
    wqb3 = w_qb.reshape(Q_LORA, MLA_HEADS, QK_NOPE + QK_ROPE)
    W["q_nope"] = wqb3[:, :, :QK_NOPE].reshape(Q_LORA, MLA_HEADS * QK_NOPE).astype(bf16)
    ra = wqb3[:, :, QK_NOPE:]
    rb = jnp.concatenate([ra[:, :, half:], ra[:, :, :half]], axis=2)
    padr = lambda r: jnp.pad(r, ((0, 0), (0, 0), (0, LANE - QK_ROPE))).reshape(Q_LORA, MLA_HEADS * LANE).astype(bf16)
    W["q_ra"], W["q_rb"] = padr(ra), padr(rb)
    W["kb_t"] = w_kb.transpose(1, 2, 0).astype(bf16)
    W["vb"] = w_vb.transpose(1, 0, 2).astype(bf16)
    W["gm_a"] = wgm[:, :D].astype(bf16)
    W["gm_b"] = wgm[:, D:].astype(bf16)
    wno4 = w_nsa_out.reshape(G, HP, DV, D)
    rows = []
    for g in range(G):
        rows.append(jnp.pad(wno4[g], ((0, 0), (g * DV, LANE - DV - g * DV), (0, 0))))
    W["nsa_out_slots"] = jnp.stack(rows, axis=0).reshape(G * HP * LANE, D).astype(bf16)
    return W


def _prompt_bias_tables(rel_bias, T):
    H = NSA_HEADS
    t = jnp.arange(T, dtype=i32)
    ends = jnp.arange(T // CMP_BLOCK, dtype=i32) * CMP_BLOCK + (CMP_BLOCK - 1)
    rel = t[None, :] - ends[:, None]
    cb = jnp.where((rel >= 0)[None], _rel_bias(rel_bias, rel), NEG)
    a = jnp.arange(NSA_KT, dtype=i32)
    r0 = a[None, :] - a[:, None]
    tz0 = jnp.where((r0 >= 0)[None], _rel_bias(rel_bias, r0), NEG)
    tz1 = _rel_bias(rel_bias, r0 + NSA_KT)
    b31 = rel_bias[NUM_BUCKETS - 1].astype(f32).reshape(NSA_GROUPS, NSA_HPG, 1)
    b31 = jnp.broadcast_to(b31, (NSA_GROUPS, NSA_HPG, Q_BLOCK)).reshape(NSA_GROUPS, 1, NSA_HPG * Q_BLOCK)
    return cb, tz0, tz1, b31


def _sample_bias_tables(rel_bias, past, wlen):
    H = NSA_HEADS
    ends = jnp.arange(past // CMP_BLOCK, dtype=i32) * CMP_BLOCK + (CMP_BLOCK - 1)
    cbs = _rel_bias(rel_bias, past - ends)
    rep = lambda v: jnp.broadcast_to(v.astype(f32)[:, None], (H, LANE))
    b31 = rep(rel_bias[NUM_BUCKETS - 1])
    b0 = rep(rel_bias[0])
    tail = jnp.arange(past - PAGE_SIZE, past, dtype=i32)
    tbl = _rel_bias(rel_bias, past - tail)
    j = jnp.arange(wlen, dtype=i32)
    d = wlen - j
    tbw = jnp.where((d < WINDOW)[None], _rel_bias(rel_bias, d), NEG)
    return cbs[:, 0::2], cbs[:, 1::2], b31, b0, tbl, tbw


def _unpad_k(kp, lead):
    return kp.reshape(-1, NSA_GROUPS, LANE)[:, :, :NSA_DK].reshape(*lead, NSA_GROUPS, NSA_DK)


def kernel(x_prompt, x_sample, cache_mla_ckv, cache_mla_krope, cache_nsa_cmp_k, cache_nsa_cmp_v, cache_nsa_sel_k, cache_nsa_sel_v, state_win_k, state_win_v, page_table, p_prompt, p_sample, rel_bias, g_ffn1, w_ffn1_gate, w_ffn1_up, w_ffn1_down, g_mix, w_in, g_q, w_qb, g_kv, w_kb, w_vb, w_nsa_out, w_mla_out, w_o, g_ffn2, w_ffn2_gate, w_ffn2_up, w_ffn2_down, g_ple, w_ple_gate, w_ple, g_final):
    assert DEPTH == 1 and DEC_SEQ == 1 and PAST_LEN % PAGE_SIZE == 0 and PAST_LEN >= PAGE_SIZE
    nb, T, D = x_prompt.shape
    ns = x_sample.shape[0]
    G, HP, DK, DV = NSA_GROUPS, NSA_HPG, NSA_DK, NSA_DV
    L = 0
    W = _prep_weights(w_in[L], w_qb[L], w_kb[L], w_vb[L], w_nsa_out[L])
    c16 = lambda a: a.astype(bf16)
    f1 = (c16(w_ffn1_gate[L]), c16(w_ffn1_up[L]), c16(w_ffn1_down[L]))
    f2 = (c16(w_ffn2_gate[L]), c16(w_ffn2_up[L]), c16(w_ffn2_down[L]))
    wno, wmo, wo = c16(w_nsa_out[L]), c16(w_mla_out[L]), c16(w_o[L])
    wpg, wpp = c16(w_ple_gate[L]), c16(w_ple[L])

    Mp = nb * T
    tm = min(512, T)
    x0 = x_prompt.reshape(Mp, D)
    x1, h_mix = _ffn(x0, g_ffn1[L], *f1, g_mix[L], tm)
    (kc_p, vc_p, ks_p, vs_p, kw_p, vw_p, ks_b, kw_b, vs_t, vw_t, gates_t, kcb, vcb) = _proj_kv(h_mix, W["kv_main"], W["kv_t"], tm)
    q_pad = _mm(h_mix, W["q"], bf16, tm, 512, "proj_q")
    cosp, sinp = _rope_tables(jnp.arange(T, dtype=i32))
    tmq = min(256, T)
    cqn, ckv_p, kr_p, kcat = _mla_proj(h_mix, W["mla_c"], g_q[L], g_kv[L], cosp, sinp, tmq)
    qcat = _mla_q(cqn, W["q_nope"], W["q_ra"], W["q_rb"], W["kb_t"], cosp, sinp, nb, tmq)
    v_p = _mla_attn(qcat, kcat, W["vb"])
    cb, tz0, tz1, b31 = _prompt_bias_tables(rel_bias, T)
    n_cmp = T // CMP_BLOCK
    vcbt = vcb.reshape(nb, n_cmp, G, DV).transpose(0, 2, 3, 1)
    o_nsa = _nsa_prompt(q_pad, kcb, vcbt, cb, ks_b, vs_t, kw_b, vw_t, gates_t.reshape(G, 3 * HP, Mp), tz0, tz1, b31, nb)
    x2 = _merge(x1, h_mix, o_nsa, v_p, W["gm_a"], W["gm_b"], wno, wmo, wo, tm)
    x3, h_ple = _ffn(x2, g_ffn2[L], *f2, g_ple[L], tm)
    y_p = _ple(x3, h_ple, p_prompt[L].reshape(Mp, -1), wpg, wpp, g_final, min(256, T)).reshape(nb, T, D)

    wl = min(WINDOW, T)
    lead = (1, nb, T)
    rows_p = (ckv_p.reshape(1, nb, T, KV_LORA), kr_p[:, :QK_ROPE].reshape(1, nb, T, QK_ROPE),
              _unpad_k(kc_p, lead), vc_p.reshape(1, nb, T, G, DV), _unpad_k(ks_p, lead), vs_p.reshape(1, nb, T, G, DV),
              _unpad_k(kw_p, lead)[:, :, T - wl:], vw_p.reshape(1, nb, T, G, DV)[:, :, T - wl:])

    xs0 = x_sample.reshape(ns, D)
    xs1, hs_mix = _ffn(xs0, g_ffn1[L], *f1, g_mix[L], ns)
    (kc_s, vc_s, ks_s, vs_s, kw_s, vw_s, _, _, _, _, _, _, _) = _proj_kv(hs_mix, W["kv_main"], W["kv_t"], ns)
    q_s2 = _mm(hs_mix, W["q_slots"], bf16, ns, 512, "proj_q_slots")
    coss, sins = _rope_tables(jnp.full((ns,), PAST_LEN, i32))
    cqn_s, ckv_s, kr_s, kcat_s = _mla_proj(hs_mix, W["mla_c"], g_q[L], g_kv[L], coss, sins, ns)
    qcat_s = _mla_q(cqn_s, W["q_nope"], W["q_ra"], W["q_rb"], W["kb_t"], coss, sins, 1, ns)
    n_pool = cache_nsa_cmp_k.shape[1]
    flat = lambda c: c.reshape(n_pool, PAGE_SIZE, -1)
    o_lat = _mla_sample(page_table, qcat_s[0].transpose(1, 0, 2), kcat_s, flat(cache_mla_ckv), flat(cache_mla_krope))
    v_s = _vb_sample(o_lat.transpose(1, 0, 2), W["vb"])

    wlen = state_win_k.shape[2]
    cbe, cbo, b31s, b0s, tbl, tbw = _sample_bias_tables(rel_bias, PAST_LEN, wlen)
    o_cmp, imp = _cmp_sample(page_table, q_s2, cbe, cbo, flat(cache_nsa_cmp_k), flat(cache_nsa_cmp_v))
    idx = _select_sample(imp.transpose(1, 0, 2))
    idx_s = idx.transpose(2, 0, 1).reshape(ns, G * TOP_N)
    compact = lambda kp: jnp.pad(kp.reshape(ns, G, LANE)[:, :, :DK].reshape(ns, G * DK), ((0, 0), (0, 2 * LANE - G * DK)))
    o_sel, o_win = _selwin_sample(page_table, idx_s, q_s2, compact(ks_s), vs_s, compact(kw_s), vw_s,
                                  state_win_k.reshape(ns, wlen, G * DK), state_win_v.reshape(ns, wlen, G * DV),
                                  b31s, b0s, tbl, tbw, flat(cache_nsa_sel_k), flat(cache_nsa_sel_v))
    slots = G * HP * LANE
    o_nsa_s = _combine_sample(hs_mix, W["gate_rep"], o_cmp.reshape(ns, slots), o_sel.reshape(ns, slots),
                              o_win.reshape(ns, slots))
    xs2 = _merge(xs1, hs_mix, o_nsa_s, v_s, W["gm_a"], W["gm_b"], W["nsa_out_slots"], wmo, wo, ns)
    xs3, hs_ple = _ffn(xs2, g_ffn2[L], *f2, g_ple[L], ns)
    y_s = _ple(xs3, hs_ple, p_sample[L].reshape(ns, -1), wpg, wpp, g_final, ns).reshape(ns, 1, D)

    lead_s = (1, ns, 1)
    kw_new = _unpad_k(kw_s, lead_s)[0]
    vw_new = vw_s.reshape(ns, 1, G, DV)
    win_k_out = jnp.concatenate([state_win_k[L], kw_new], axis=1)[:, 1:][None]
    win_v_out = jnp.concatenate([state_win_v[L], vw_new], axis=1)[:, 1:][None]
    rows_s = (ckv_s.reshape(1, ns, 1, KV_LORA), kr_s[:, :QK_ROPE].reshape(1, ns, 1, QK_ROPE),
              _unpad_k(kc_s, lead_s), vc_s.reshape(1, ns, 1, G, DV), _unpad_k(ks_s, lead_s), vs_s.reshape(1, ns, 1, G, DV),
              win_k_out, win_v_out)
    return (y_p, y_s) + rows_p + rows_s
```
